```python
import functools
import jax
import jax.numpy as jnp
from jax import lax
import numpy as np

D_MODEL = 2048
BATCH = 4
SEQ = 8192
DEPTH = 1

CTX_LEN = 256
GRID_W = 64
CHUNK = 128
EPS = 1e-6
ADA_CHUNKS = 6

R_HEADS = 8
R_DK = 64
R_DV = 128
R_QK = R_HEADS * R_DK
R_WIDTH = R_HEADS * R_DV
ROPE_BASE = 10000.0

M_HEADS = 4
M_DK = 128
M_DV = 256
M_QK = M_HEADS * M_DK
M_WIDTH = M_HEADS * M_DV
M_CONV = 3
M_N_GATES = 2 * 2 * M_HEADS

P_HEADS = 8
P_NKEYS = 128
P_EXPERTS = P_NKEYS * P_NKEYS
P_DQ = 256
P_TOPK = 16
P_BLOCK = 128

IN_SPLITS = (R_QK, R_QK, R_WIDTH, R_WIDTH, M_QK, M_QK, M_WIDTH, M_WIDTH, M_N_GATES, D_MODEL, D_MODEL)
IN_COLS = 2 * R_QK + 2 * R_WIDTH + 2 * M_QK + 2 * M_WIDTH + M_N_GATES + 2 * D_MODEL

kernel_name = 'hybrid_retention_mlstm_peer_dit'


def _split_in(p):
    out, off = [], 0
    for size in IN_SPLITS:
        out.append(p[..., off:off + size])
        off += size
    return out


def _rmsnorm(x, g):
    xf = x.astype(jnp.float32)
    y = xf * lax.rsqrt(jnp.mean(xf * xf, axis=-1, keepdims=True) + EPS)
    return (y * g.astype(jnp.float32)).astype(x.dtype)


def _modulate(x, g, shift, scale):
    return _rmsnorm(x, g) * (1.0 + scale) + shift


def _head_norm(y):
    yc = y - jnp.mean(y, axis=-1, keepdims=True)
    return yc * lax.rsqrt(jnp.mean(yc * yc, axis=-1, keepdims=True) + EPS)


def _rotary_2d(t, row, col):
    half = t.shape[-1] // 2
    n_ax = half // 2
    freq = ROPE_BASE ** (-jnp.arange(n_ax, dtype=jnp.float32) / n_ax)
    ang = jnp.concatenate([row[:, None] * freq, col[:, None] * freq], axis=-1)
    cos = jnp.cos(ang)[None, :, None, :]
    sin = jnp.sin(ang)[None, :, None, :]
    t1, t2 = t[..., :half], t[..., half:]
    return jnp.concatenate([t1 * cos - t2 * sin, t2 * cos + t1 * sin], axis=-1)


def _to_chunks(t):
    bsz, length = t.shape[:2]
    t = t.reshape((bsz, length // CHUNK, CHUNK) + t.shape[2:])
    return jnp.swapaxes(jnp.moveaxis(t, 1, 0), 2, 3)


def _from_chunks(t):
    t = jnp.moveaxis(jnp.swapaxes(t, 2, 3), 0, 1)
    return t.reshape((t.shape[0], t.shape[1] * t.shape[2]) + t.shape[3:])


def _flip(ts):
    return tuple(jnp.flip(t, axis=1) for t in ts)


def _bidirectional(scan_f, scan_b, lat_f, lat_b, ctx_f, ctx_b, init, need_ctx):
    st_f, yc_f = scan_f(ctx_f, init, need_ctx)
    st_b, yc_b = scan_b(_flip(ctx_b), init, need_ctx)
    _, yl_f = scan_f(lat_f, st_f, True)
    _, yl_b = scan_b(_flip(lat_b), st_b, True)
    y_lat = yl_f + jnp.flip(yl_b, axis=1)
    y_ctx = yc_f + jnp.flip(yc_b, axis=1) if need_ctx else None
    return y_lat, y_ctx


def _retention_scan(seqs, s0, with_out, log_gamma):
    q, k, v = (_to_chunks(t) for t in seqs)
    pos = jnp.arange(CHUNK, dtype=jnp.float32)
    lg = log_gamma[:, None]
    rel = pos[:, None] - pos[None, :]
    intra = jnp.where(rel >= 0, jnp.exp(lg[:, :, None] * jnp.maximum(rel, 0.0)), 0.0)
    q_dec = jnp.exp(lg * (pos + 1.0))[None, :, :, None]
    k_dec = jnp.exp(lg * (CHUNK - 1.0 - pos))[None, :, :, None]
    c_dec = jnp.exp(lg * CHUNK)[None, :, :, None]

    def step(s, inp):
        qc, kc, vc = inp
        s_new = c_dec * s + jnp.einsum('bhcd,bhce->bhde', kc * k_dec, vc)
        if not with_out:
            return s_new, None
        att = jnp.einsum('bhcd,bhsd->bhcs', qc, kc) * intra
        y = jnp.einsum('bhcs,bhse->bhce', att, vc) + jnp.einsum('bhcd,bhde->bhce', qc * q_dec, s)
        return s_new, y

    s_fin, ys = lax.scan(step, s0, (q, k, v))
    return s_fin, (_from_chunks(ys) if with_out else None)


def _mlstm_scan(seqs, state0, with_out):
    q, k, v, ig, lf = (_to_chunks(t) for t in seqs)
    tri = jnp.tril(jnp.ones((CHUNK, CHUNK), dtype=bool))

    def step(carry, inp):
        c_prev, n_prev, m_prev = carry
        qc, kc, vc, ic, fc = inp
        b = jnp.cumsum(fc, axis=-1)
        b_end = b[..., -1]
        w_log = b_end[..., None] - b + ic
        a_end = b_end + m_prev
        m_new = jnp.maximum(a_end, jnp.max(w_log, axis=-1))
        p = jnp.exp(a_end - m_new)
        kw = kc * jnp.exp(w_log - m_new[..., None])[..., None]
        c_new = p[..., None, None] * c_prev + jnp.einsum('bhcd,bhce->bhde', kw, vc)
        n_new = p[..., None] * n_prev + jnp.sum(kw, axis=2)
        new = (c_new, n_new, m_new)
        if not with_out:
            return new, None
        a = b + m_prev[..., None]
        d_log = jnp.where(tri, b[..., :, None] - b[..., None, :] + ic[..., None, :], -jnp.inf)
        m_t = jnp.maximum(a, jnp.max(d_log, axis=-1))
        att = jnp.einsum('bhcd,bhsd->bhcs', qc, kc) * jnp.exp(d_log - m_t[..., None])
        inter = jnp.exp(a - m_t)[..., None]
        num = jnp.einsum('bhcs,bhse->bhce', att, vc) + inter * jnp.einsum('bhcd,bhde->bhce', qc, c_prev)
        den = jnp.sum(att, axis=-1, keepdims=True) + inter * jnp.einsum('bhcd,bhd->bhc', qc, n_prev)[..., None]
        h = num / jnp.maximum(jnp.abs(den), jnp.exp(-m_t)[..., None])
        return new, h

    st, hs = lax.scan(step, state0, (q, k, v, ig, lf))
    return st, (_from_chunks(hs) if with_out else None)


def _retention_inputs(q, k, v, row, col):
    bsz, length = q.shape[:2]
    qh = q.reshape(bsz, length, R_HEADS, R_DK).astype(jnp.float32)
    kh = k.reshape(bsz, length, R_HEADS, R_DK).astype(jnp.float32)
    if row is not None:
        qh = _rotary_2d(qh, row, col)
        kh = _rotary_2d(kh, row, col)
    vh = v.reshape(bsz, length, R_HEADS, R_DV).astype(jnp.float32)
    return (qh, kh * (R_DK ** -0.5), vh)


def _dwconv_centred(x, w):
    ksz, ch = w.shape
    return lax.conv_general_dilated(x, w[:, None, :].astype(x.dtype), window_strides=(1,),
                                    padding=[(ksz // 2, ksz - 1 - ksz // 2)],
                                    dimension_numbers=('NWC', 'WIO', 'NWC'), feature_group_count=ch)


def _mlstm_inputs(q, k, v, gates, conv_w, gate_bias):
    bsz, length = q.shape[:2]
    qk = jax.nn.silu(_dwconv_centred(jnp.concatenate([q, k], axis=-1), conv_w))
    qh = qk[..., :M_QK].reshape(bsz, length, M_HEADS, M_DK).astype(jnp.float32)
    kh = qk[..., M_QK:].reshape(bsz, length, M_HEADS, M_DK).astype(jnp.float32) * (M_DK ** -0.5)
    vh = v.reshape(bsz, length, M_HEADS, M_DV).astype(jnp.float32)
    g = gates.astype(jnp.float32).reshape(bsz, length, 2, 2, M_HEADS) + gate_bias.astype(jnp.float32)
    fwd = (qh, kh, vh, g[:, :, 0, 0], jax.nn.log_sigmoid(g[:, :, 0, 1]))
    bwd = (qh, kh, vh, g[:, :, 1, 0], jax.nn.log_sigmoid(g[:, :, 1, 1]))
    return fwd, bwd


def _retention_out(y, gate):
    bsz, length = y.shape[:2]
    return _head_norm(y).reshape(bsz, length, R_WIDTH).astype(gate.dtype) * jax.nn.silu(gate)


def _mlstm_out(h, o):
    bsz, length = h.shape[:2]
    return _head_norm(h).reshape(bsz, length, M_WIDTH).astype(o.dtype) * jax.nn.sigmoid(o)


def _merge(yr, ym, g_r, g_m, w_ret_out, w_mlstm_out, w_out):
    merged = jax.nn.sigmoid(g_r) * (yr @ w_ret_out) + jax.nn.sigmoid(g_m) * (ym @ w_mlstm_out)
    return merged @ w_out


def _peer(h, w_query, sub_keys, expert_down, expert_up):
    bsz, length, dm = h.shape
    tokens = h.reshape(-1, P_BLOCK, dm)

    def block(tb):
        qh = (tb @ w_query).reshape(P_BLOCK, P_HEADS, 2, P_DQ // 2)
        s = jnp.einsum('thpd,hpkd->thpk', qh, sub_keys).astype(jnp.float32)
        sv, si = lax.top_k(s, P_TOPK)
        cand_s = (sv[:, :, 0, :, None] + sv[:, :, 1, None, :]).reshape(P_BLOCK, P_HEADS, P_TOPK * P_TOPK)
        cand_i = (si[:, :, 0, :, None] * P_NKEYS + si[:, :, 1, None, :]).reshape(P_BLOCK, P_HEADS, P_TOPK * P_TOPK)
        best_s, best_j = lax.top_k(cand_s, P_TOPK)
        eid = jnp.take_along_axis(cand_i, best_j, axis=-1)
        g = jax.nn.softmax(best_s, axis=-1)
        act = jax.nn.gelu(jnp.einsum('thkd,td->thk', expert_down[eid], tb).astype(jnp.float32), approximate=False)
        return jnp.einsum('thk,thkd->td', (g * act).astype(tb.dtype), expert_up[eid])

    return lax.map(block, tokens).reshape(bsz, length, dm)


def _layer(x, ctx, c, c_ctx, row, col, w_ada, b_ada, norm1_g, w_in, ret_decay, m_conv, m_gate_bias,
           w_ret_out, w_mlstm_out, w_out, norm2_g, peer_query, peer_keys, peer_down, peer_up, update_ctx):
    f32 = jnp.float32
    bsz = x.shape[0]
    sh1, sc1, gt1, sh2, sc2, gt2 = jnp.split((jax.nn.silu(c) @ w_ada + b_ada)[:, None, :], ADA_CHUNKS, axis=-1)
    csh1, csc1, cgt1, csh2, csc2, cgt2 = jnp.split(jax.nn.silu(c_ctx) @ w_ada + b_ada, ADA_CHUNKS, axis=-1)
    rq, rk, rv, rg, mq, mk, mv, mo, mg, gr, gm = _split_in(_modulate(x, norm1_g, sh1, sc1) @ w_in)
    crq, crk, crv, crg, cmq, cmk, cmv, cmo, cmg, cgr, cgm = _split_in(_modulate(ctx, norm1_g, csh1, csc1) @ w_in)

    log_gamma = jax.nn.log_sigmoid(ret_decay.astype(f32))
    r_lat = _retention_inputs(rq, rk, rv, row, col)
    r_ctx = _retention_inputs(crq, crk, crv, None, None)
    s0 = jnp.zeros((bsz, R_HEADS, R_DK, R_DV), f32)
    yr, yr_c = _bidirectional(functools.partial(_retention_scan, log_gamma=log_gamma[0]),
                              functools.partial(_retention_scan, log_gamma=log_gamma[1]),
                              r_lat, r_lat, r_ctx, r_ctx, s0, update_ctx)

    m_lat_f, m_lat_b = _mlstm_inputs(mq, mk, mv, mg, m_conv, m_gate_bias)
    m_ctx_f, m_ctx_b = _mlstm_inputs(cmq, cmk, cmv, cmg, m_conv, m_gate_bias)
    m0 = (jnp.zeros((bsz, M_HEADS, M_DK, M_DV), f32), jnp.zeros((bsz, M_HEADS, M_DK), f32),
          jnp.zeros((bsz, M_HEADS), f32))
    ym, ym_c = _bidirectional(_mlstm_scan, _mlstm_scan, m_lat_f, m_lat_b, m_ctx_f, m_ctx_b, m0, update_ctx)

    x = x + gt1 * _merge(_retention_out(yr, rg), _mlstm_out(ym, mo), gr, gm, w_ret_out, w_mlstm_out, w_out)
    x = x + gt2 * _peer(_modulate(x, norm2_g, sh2, sc2), peer_query, peer_keys, peer_down, peer_up)
    if update_ctx:
        ctx = ctx + cgt1 * _merge(_retention_out(yr_c, crg), _mlstm_out(ym_c, cmo), cgr, cgm,
                                  w_ret_out, w_mlstm_out, w_out)
        ctx = ctx + cgt2 * _peer(_modulate(ctx, norm2_g, csh2, csc2), peer_query, peer_keys, peer_down, peer_up)
    return x, ctx


def setup_inputs(seed: int = 0) -> dict:
    key = jax.random.key(seed)
    ks = jax.random.split(key, 24)
    f32 = jnp.float32

    def nrm(k, shape, scale):
        return jax.random.normal(k, shape, f32) * scale

    x = nrm(ks[0], (BATCH, SEQ, D_MODEL), 1.0)
    c = nrm(ks[1], (BATCH, D_MODEL), 1.0)
    ctx = nrm(ks[2], (BATCH, CTX_LEN, D_MODEL), 1.0)
    c_ctx = nrm(ks[3], (D_MODEL,), 1.0)
    w_ada = nrm(ks[4], (DEPTH, D_MODEL, ADA_CHUNKS * D_MODEL), D_MODEL ** -0.5)
    b_ada = nrm(ks[5], (DEPTH, ADA_CHUNKS * D_MODEL), 0.02)
    norm1_g = 1.0 + nrm(ks[6], (DEPTH, D_MODEL), 0.02)
    w_in = nrm(ks[7], (DEPTH, D_MODEL, IN_COLS), D_MODEL ** -0.5)
    decay_logit = jnp.log(2.0 ** (5.0 + jnp.arange(R_HEADS, dtype=f32)) - 1.0)
    ret_decay = decay_logit[None, None, :] + nrm(ks[8], (DEPTH, 2, R_HEADS), 0.05)
    m_conv = nrm(ks[9], (DEPTH, M_CONV, 2 * M_QK), M_CONV ** -0.5)
    gate_base = jnp.stack([jnp.zeros((M_HEADS,), f32), jnp.linspace(3.0, 6.0, M_HEADS, dtype=f32)])
    m_gate_bias = gate_base[None, None] + nrm(ks[10], (DEPTH, 2, 2, M_HEADS), 0.1)
    w_ret_out = nrm(ks[11], (DEPTH, R_WIDTH, D_MODEL), R_WIDTH ** -0.5)
    w_mlstm_out = nrm(ks[12], (DEPTH, M_WIDTH, D_MODEL), M_WIDTH ** -0.5)
    w_out = nrm(ks[13], (DEPTH, D_MODEL, D_MODEL), D_MODEL ** -0.5)
    norm2_g = 1.0 + nrm(ks[14], (DEPTH, D_MODEL), 0.02)
    peer_query = nrm(ks[15], (DEPTH, D_MODEL, P_HEADS * P_DQ), D_MODEL ** -0.5)
    peer_keys = nrm(ks[16], (DEPTH, P_HEADS, 2, P_NKEYS, P_DQ // 2), (P_DQ // 2) ** -0.5)
    peer_down = nrm(ks[17], (DEPTH, P_EXPERTS, D_MODEL), D_MODEL ** -0.5)
    peer_up = nrm(ks[18], (DEPTH, P_EXPERTS, D_MODEL), P_HEADS ** -0.5)
    final_g = 1.0 + nrm(ks[19], (D_MODEL,), 0.02)
    return {'x': x, 'c': c, 'ctx': ctx, 'c_ctx': c_ctx, 'w_ada': w_ada, 'b_ada': b_ada,
            'norm1_g': norm1_g, 'w_in': w_in, 'ret_decay': ret_decay, 'm_conv': m_conv,
            'm_gate_bias': m_gate_bias, 'w_ret_out': w_ret_out, 'w_mlstm_out': w_mlstm_out,
            'w_out': w_out, 'norm2_g': norm2_g, 'peer_query': peer_query, 'peer_keys': peer_keys,
            'peer_down': peer_down, 'peer_up': peer_up, 'final_g': final_g}


def reference(x, c, ctx, c_ctx, w_ada, b_ada, norm1_g, w_in, ret_decay, m_conv, m_gate_bias,
              w_ret_out, w_mlstm_out, w_out, norm2_g, peer_query, peer_keys, peer_down, peer_up, final_g):
    length = x.shape[1]
    ROWS = length // GRID_W
    row = jnp.repeat(jnp.arange(ROWS, dtype=jnp.float32), GRID_W)
    col = jnp.tile(jnp.arange(GRID_W, dtype=jnp.float32), ROWS)
    for layer in range(DEPTH):
        x, ctx = _layer(x, ctx, c, c_ctx, row, col, w_ada[layer], b_ada[layer], norm1_g[layer], w_in[layer],
                        ret_decay[layer], m_conv[layer], m_gate_bias[layer], w_ret_out[layer],
                        w_mlstm_out[layer], w_out[layer], norm2_g[layer], peer_query[layer],
                        peer_keys[layer], peer_down[layer], peer_up[layer], update_ctx=layer < DEPTH - 1)
    return _rmsnorm(x, final_g)
```

```python
import functools

import jax
import jax.numpy as jnp
from jax import lax
from jax.experimental import pallas as pl
from jax.experimental.pallas import tpu as pltpu

F32 = jnp.float32
BF16 = jnp.bfloat16

EPS = 1e-6
CHUNK = 128
GRID_W = 64
ROPE_BASE = 10000.0
ADA_CHUNKS = 6

R_HEADS, R_DK, R_DV = 8, 64, 128
M_HEADS, M_DK, M_DV = 4, 128, 256
R_QK, R_WIDTH = R_HEADS * R_DK, R_HEADS * R_DV
M_QK, M_WIDTH = M_HEADS * M_DK, M_HEADS * M_DV
M_N_GATES = 2 * 2 * M_HEADS

P_HEADS, P_NKEYS, P_TOPK = 8, 128, 16
P_EXPERTS = P_NKEYS * P_NKEYS

LANES = 128
MIB = 1024 * 1024
VMEM_BUDGET_BYTES = 56 * MIB

STATE_COLS = 2 * R_QK + R_WIDTH + 2 * M_QK + M_WIDTH
SEL_ROWS = 2 * P_NKEYS + 8


def _params(semantics, vmem_bytes):
    return pltpu.CompilerParams(dimension_semantics=semantics,
                                vmem_limit_bytes=int(min(VMEM_BUDGET_BYTES, vmem_bytes)))


def _log_sigmoid(x):
    return jnp.minimum(x, 0.0) - jnp.log1p(jnp.exp(-jnp.abs(x)))


def _sigmoid(x):
    return 1.0 / (1.0 + jnp.exp(-x))


def _dot(a, b):
    return jnp.dot(a, b, preferred_element_type=F32)


def _dot_nt(a, b):
    return lax.dot_general(a, b, (((1,), (1,)), ((), ())), preferred_element_type=F32)


def _group_norm(y, group):
    outs = []
    for g in range(y.shape[1] // group):
        yg = y[:, g * group:(g + 1) * group]
        yc = yg - jnp.mean(yg, axis=-1, keepdims=True)
        outs.append(yc * lax.rsqrt(jnp.mean(yc * yc, axis=-1, keepdims=True) + EPS))
    return jnp.concatenate(outs, axis=1)


def _ada_kernel(c_ref, w_ref, b_ref, o_ref):
    c = c_ref[...]
    a = c * _sigmoid(c)
    o_ref[...] = jnp.dot(a, w_ref[...], precision=lax.Precision.HIGHEST, preferred_element_type=F32) + b_ref[...]


def _ada(c_rows, w_ada, b_ada):
    rows, d = c_rows.shape
    n = w_ada.shape[1]
    tn = 1024
    return pl.pallas_call(
        _ada_kernel,
        grid=(n // tn,),
        in_specs=[pl.BlockSpec((rows, d), lambda j: (0, 0)),
                  pl.BlockSpec((d, tn), lambda j: (0, j)),
                  pl.BlockSpec((1, tn), lambda j: (0, j))],
        out_specs=pl.BlockSpec((rows, tn), lambda j: (0, j)),
        out_shape=jax.ShapeDtypeStruct((rows, n), F32),
        compiler_params=_params(("arbitrary",), 2 * d * tn * 4 + 8 * MIB),
        name="ada",
    )(c_rows, w_ada, b_ada.reshape(1, n))


def _rope(t, cos, sin_signed):
    lane = lax.broadcasted_iota(jnp.int32, (1, LANES), 1)
    first_half = jnp.bitwise_and(lane, R_DK - 1) < (R_DK // 2)
    outs = []
    for g in range(t.shape[1] // LANES):
        tg = t[:, g * LANES:(g + 1) * LANES]
        partner = jnp.where(first_half,
                            pltpu.roll(tg, LANES - R_DK // 2, axis=1),
                            pltpu.roll(tg, R_DK // 2, axis=1))
        outs.append(tg * cos + partner * sin_signed)
    return jnp.concatenate(outs, axis=1)


def _inproj_kernel(x_ref, g_ref, sh_ref, sc_ref, w_ref, wg_ref, *rest, rope):
    if rope:
        cos_ref, sin_ref, p_ref, gate_ref, xn_ref = rest
    else:
        p_ref, gate_ref, xn_ref = rest
    j = pl.program_id(2)

    @pl.when(j == 0)
    def _():
        x = x_ref[0]
        y = x * lax.rsqrt(jnp.mean(x * x, axis=-1, keepdims=True) + EPS) * g_ref[...]
        xn = (y * (1.0 + sc_ref[0]) + sh_ref[0]).astype(BF16)
        xn_ref[...] = xn
        gate_ref[0] = _dot(xn, wg_ref[...])

    acc = _dot(xn_ref[...], w_ref[...])
    if rope:
        @pl.when(j == 0)
        def _():
            p_ref[0] = _rope(acc, cos_ref[...], sin_ref[...]).astype(BF16)

        @pl.when(j != 0)
        def _():
            p_ref[0] = acc.astype(BF16)
    else:
        p_ref[0] = acc.astype(BF16)


def _inproj(x, g, sh, sc, w, wg, ncols, rope_tabs):
    bsz, length, d = x.shape
    tm = min(1024, length)
    tn = 2 * R_QK
    rope = rope_tabs is not None
    in_specs = [pl.BlockSpec((1, tm, d), lambda b, i, j: (b, i, 0)),
                pl.BlockSpec((1, d), lambda b, i, j: (0, 0)),
                pl.BlockSpec((1, 1, d), lambda b, i, j: (b, 0, 0)),
                pl.BlockSpec((1, 1, d), lambda b, i, j: (b, 0, 0)),
                pl.BlockSpec((d, tn), lambda b, i, j: (0, j)),
                pl.BlockSpec((d, LANES), lambda b, i, j: (0, 0))]
    args = [x, g, sh, sc, w, wg]
    if rope:
        in_specs += [pl.BlockSpec((tm, LANES), lambda b, i, j: (i, 0))] * 2
        args += list(rope_tabs)
    vmem = 2 * tm * d * 4 + 2 * d * tn * 2 + 2 * tm * tn * 2 + tm * d * 2 + 3 * tm * tn * 4 + 8 * MIB
    return pl.pallas_call(
        functools.partial(_inproj_kernel, rope=rope),
        grid=(bsz, length // tm, ncols // tn),
        in_specs=in_specs,
        out_specs=[pl.BlockSpec((1, tm, tn), lambda b, i, j: (b, i, j)),
                   pl.BlockSpec((1, tm, LANES), lambda b, i, j: (b, i, 0))],
        out_shape=[jax.ShapeDtypeStruct((bsz, length, ncols), BF16),
                   jax.ShapeDtypeStruct((bsz, length, LANES), F32)],
        scratch_shapes=[pltpu.VMEM((tm, d), BF16)],
        compiler_params=_params(("arbitrary", "arbitrary", "arbitrary"), vmem),
        name="inproj_rope" if rope else "inproj_ctx",
    )(*args)


def _ret_kernel(dec_ref, q_ref, k_ref, v_ref, sf0_ref, sb0_ref, *rest, with_out, n_chunks):
    if with_out:
        y_ref, sf, sb, yacc = rest
    else:
        sfn_ref, sbn_ref, sf, sb = rest
    pair = pl.program_id(1)
    head_a = 2 * pair
    head_b = head_a + 1
    lane = lax.broadcasted_iota(jnp.int32, (1, LANES), 1)
    lane_v = lax.broadcasted_iota(jnp.int32, (1, 2 * R_DV), 1)
    row = lax.broadcasted_iota(jnp.int32, (CHUNK, 1), 0)
    pos = row.astype(F32)
    in_a = lane < R_DK
    block_diag = (row < R_DK) == (lane_v < R_DV)

    def decay_lanes(d):
        return _log_sigmoid(jnp.where(in_a, dec_ref[d, head_a], dec_ref[d, head_b]))

    def decay_rows(d):
        return _log_sigmoid(jnp.where(row < R_DK, dec_ref[d, head_a], dec_ref[d, head_b]))

    lgf, lgb = decay_lanes(0), decay_lanes(1)
    scale = R_DK ** -0.5
    qdec_f = jnp.exp(lgf * (pos + 1.0))
    kdec_f = jnp.exp(lgf * (CHUNK - 1.0 - pos)) * scale
    qdec_b = jnp.exp(lgb * (CHUNK - pos))
    kdec_b = jnp.exp(lgb * pos) * scale
    cdec_f = jnp.exp(decay_rows(0) * CHUNK)
    cdec_b = jnp.exp(decay_rows(1) * CHUNK)

    sf[...] = sf0_ref[0, 0]
    sb[...] = sb0_ref[0, 0]

    def load(c):
        s = pl.multiple_of(c * CHUNK, CHUNK)
        return s, q_ref[0, pl.ds(s, CHUNK), :], k_ref[0, pl.ds(s, CHUNK), :], v_ref[0, pl.ds(s, CHUNK), :]

    def state_update(state_ref, k, v, kdec, cdec):
        kt = (k.astype(F32) * kdec).T.astype(BF16)
        state_ref[...] = cdec * state_ref[...] + jnp.where(block_diag, _dot(kt, v), 0.0)

    def backward_body(i, carry):
        s, q, k, v = load(n_chunks - 1 - i)
        if with_out:
            yacc[pl.ds(s, CHUNK), :] = _dot((q.astype(F32) * qdec_b).astype(BF16), sb[...].astype(BF16))
        state_update(sb, k, v, kdec_b, cdec_b)
        return carry

    lax.fori_loop(0, n_chunks, backward_body, 0)

    if with_out:
        rel = (row - lane).astype(F32)

        def intra_decay(head):
            lf = _log_sigmoid(jnp.full((1, LANES), dec_ref[0, head], F32))
            lb = _log_sigmoid(jnp.full((1, LANES), dec_ref[1, head], F32))
            fwd = jnp.where(rel >= 0, jnp.exp(lf * jnp.maximum(rel, 0.0)), 0.0)
            bwd = jnp.where(rel <= 0, jnp.exp(lb * jnp.maximum(-rel, 0.0)), 0.0)
            return (fwd + bwd) * scale

        d_a = intra_decay(head_a)
        d_b = intra_decay(head_b)

    def forward_body(c, carry):
        s, q, k, v = load(c)
        if with_out:
            qf = q.astype(F32)
            q_a = jnp.where(in_a, qf, 0.0).astype(BF16)
            q_b = jnp.where(in_a, 0.0, qf).astype(BF16)
            att_a = (_dot_nt(q_a, k) * d_a).astype(BF16)
            att_b = (_dot_nt(q_b, k) * d_b).astype(BF16)
            y = jnp.concatenate([_dot(att_a, v[:, :R_DV]), _dot(att_b, v[:, R_DV:])], axis=1)
            y = y + _dot((qf * qdec_f).astype(BF16), sf[...].astype(BF16)) + yacc[pl.ds(s, CHUNK), :]
            y_ref[0, pl.ds(s, CHUNK), :] = _group_norm(y, R_DV).astype(BF16)
        state_update(sf, k, v, kdec_f, cdec_f)
        return carry

    lax.fori_loop(0, n_chunks, forward_body, 0)

    if not with_out:
        sfn_ref[0, 0] = sf[...]
        sbn_ref[0, 0] = sb[...]


def _ret_scan(ret_decay, p, sf0, sb0, with_out):
    bsz, length, _ = p.shape
    n_pairs = R_HEADS // 2
    k_blk0 = R_QK // LANES
    v_blk0 = 2 * R_QK // (2 * R_DV)
    state_spec = pl.BlockSpec((1, 1, 2 * R_DK, 2 * R_DV), lambda b, h: (b, h, 0, 0))
    state_shape = jax.ShapeDtypeStruct((bsz, n_pairs, 2 * R_DK, 2 * R_DV), F32)
    in_specs = [pl.BlockSpec(memory_space=pltpu.SMEM),
                pl.BlockSpec((1, length, LANES), lambda b, h: (b, 0, h)),
                pl.BlockSpec((1, length, LANES), lambda b, h: (b, 0, k_blk0 + h)),
                pl.BlockSpec((1, length, 2 * R_DV), lambda b, h: (b, 0, v_blk0 + h)),
                state_spec, state_spec]
    scratch = [pltpu.VMEM((2 * R_DK, 2 * R_DV), F32)] * 2
    seq_bytes = length * (2 * LANES + 2 * R_DV) * 2
    if with_out:
        out_specs = pl.BlockSpec((1, length, 2 * R_DV), lambda b, h: (b, 0, h))
        out_shape = jax.ShapeDtypeStruct((bsz, length, R_WIDTH), BF16)
        scratch = scratch + [pltpu.VMEM((length, 2 * R_DV), F32)]
        vmem = 2 * seq_bytes + 2 * length * 2 * R_DV * 2 + length * 2 * R_DV * 4 + 8 * MIB
    else:
        out_specs = [state_spec, state_spec]
        out_shape = [state_shape, state_shape]
        vmem = 2 * seq_bytes + 8 * MIB
    return pl.pallas_call(
        functools.partial(_ret_kernel, with_out=with_out, n_chunks=length // CHUNK),
        grid=(bsz, n_pairs),
        in_specs=in_specs,
        out_specs=out_specs,
        out_shape=out_shape,
        scratch_shapes=scratch,
        compiler_params=_params(("arbitrary", "arbitrary"), vmem),
        name="ret_scan" if with_out else "ret_state",
    )(ret_decay, p, p, p, sf0, sb0)


HALO = 16


def _mlstm_kernel(q_ref, k_ref, v_ref, g_ref, cwq_ref, cwk_ref, gb_ref,
                  c0f_ref, c0b_ref, n0f_ref, n0b_ref, m0f_ref, m0b_ref, *rest, with_out, n_chunks):
    if with_out:
        y_ref, qc, kc, c_s, n_s, m_s, yacc = rest
    else:
        cfn_ref, cbn_ref, nfn_ref, nbn_ref, mfn_ref, mbn_ref, qc, kc, c_s, n_s, m_s = rest
    head = pl.program_id(1)
    length = n_chunks * CHUNK
    lane = lax.broadcasted_iota(jnp.int32, (1, LANES), 1)
    row = lax.broadcasted_iota(jnp.int32, (CHUNK, 1), 0)
    neg_inf = jnp.float32(-jnp.inf)

    def conv_body(c, carry):
        s = pl.multiple_of(c * CHUNK, CHUNK)
        sp = pl.multiple_of(jnp.maximum(s - HALO, 0), HALO)
        sn = pl.multiple_of(jnp.minimum(s + CHUNK, length - HALO), HALO)
        has_prev = (c > 0).astype(F32)
        has_next = (c < n_chunks - 1).astype(F32)
        for src, w_ref, dst, out_scale in ((q_ref, cwq_ref, qc, 1.0), (k_ref, cwk_ref, kc, M_DK ** -0.5)):
            xw = jnp.concatenate([src[0, pl.ds(sp, HALO), :].astype(F32) * has_prev,
                                  src[0, pl.ds(s, CHUNK), :].astype(F32),
                                  src[0, pl.ds(sn, HALO), :].astype(F32) * has_next], axis=0)
            rows = CHUNK + 2 * HALO
            w = w_ref[...]
            y = (w[0:1] * pltpu.roll(xw, 1, axis=0) + w[1:2] * xw + w[2:3] * pltpu.roll(xw, rows - 1, axis=0))
            y = y[HALO:HALO + CHUNK]
            dst[pl.ds(s, CHUNK), :] = (y * _sigmoid(y) * out_scale).astype(BF16)
        return carry

    lax.fori_loop(0, n_chunks, conv_body, 0)

    def column(mat, idx):
        return jnp.sum(jnp.where(lane == idx, mat, 0.0), axis=-1, keepdims=True)

    def step(c, forward):
        s = pl.multiple_of(c * CHUNK, CHUNK)
        q = qc[pl.ds(s, CHUNK), :]
        k = kc[pl.ds(s, CHUNK), :]
        v = v_ref[0, pl.ds(s, CHUNK), :]
        gates = g_ref[0, pl.ds(s, CHUNK), :] + gb_ref[...]
        logf = _log_sigmoid(gates)
        csum = logf
        for sh in (1, 2, 4, 8, 16, 32, 64):
            csum = csum + jnp.where(row >= sh, pltpu.roll(csum, sh, axis=0), 0.0)
        if forward:
            b_col = column(csum, M_HEADS + head)
            i_col = column(gates, head)
            end_row = CHUNK - 1
            keep = row >= lane
        else:
            total = csum[CHUNK - 1:CHUNK, :]
            b_col = column(total - csum + logf, 3 * M_HEADS + head)
            i_col = column(gates, 2 * M_HEADS + head)
            end_row = 0
            keep = lane >= row
        b_mat = jnp.broadcast_to(b_col, (CHUNK, LANES))
        u_col = i_col - b_col
        u_row = jnp.where(lane == 0, b_col, jnp.where(lane == 1, u_col, 0.0)).T[1:2, :]
        b_end = jnp.sum(jnp.where(row == end_row, b_mat, 0.0), axis=0, keepdims=True)

        c_prev = c_s[...]
        n_prev = n_s[...]
        m_prev = m_s[...]
        w_log = b_end + u_col
        a_end = b_end + m_prev
        m_new = jnp.maximum(a_end, jnp.max(w_log, axis=0, keepdims=True))
        decay = jnp.exp(a_end - m_new)
        kw = k.astype(F32) * jnp.exp(w_log - m_new)
        c_s[...] = jnp.concatenate([decay, decay], axis=1) * c_prev + _dot(kw.T.astype(BF16), v)
        n_s[...] = decay * n_prev + jnp.sum(kw, axis=0, keepdims=True)
        m_s[...] = m_new
        if not with_out:
            return None
        a_mat = b_mat + m_prev
        d_log = jnp.where(keep, b_col + u_row, neg_inf)
        m_t = jnp.max(jnp.maximum(d_log, a_mat), axis=-1, keepdims=True)
        att = _dot_nt(q, k) * jnp.exp(d_log - m_t)
        inter = jnp.exp(jnp.max(a_mat, axis=-1, keepdims=True) - m_t)
        num = _dot(att.astype(BF16), v) + inter * _dot(q, c_prev.astype(BF16))
        den = (jnp.sum(att, axis=-1, keepdims=True)
               + inter * jnp.sum(q.astype(F32) * n_prev, axis=-1, keepdims=True))
        return s, num / jnp.maximum(jnp.abs(den), jnp.exp(-m_t))

    def backward_body(i, carry):
        out = step(n_chunks - 1 - i, forward=False)
        if with_out:
            yacc[pl.ds(out[0], CHUNK), :] = out[1]
        return carry

    def forward_body(c, carry):
        out = step(c, forward=True)
        if with_out:
            s, h = out
            y_ref[0, pl.ds(s, CHUNK), :] = _group_norm(h + yacc[pl.ds(s, CHUNK), :], M_DV).astype(BF16)
        return carry

    c_s[...] = c0b_ref[0, 0]
    n_s[...] = n0b_ref[0, 0]
    m_s[...] = m0b_ref[0, 0]
    lax.fori_loop(0, n_chunks, backward_body, 0)
    if not with_out:
        cbn_ref[0, 0] = c_s[...]
        nbn_ref[0, 0] = n_s[...]
        mbn_ref[0, 0] = m_s[...]

    c_s[...] = c0f_ref[0, 0]
    n_s[...] = n0f_ref[0, 0]
    m_s[...] = m0f_ref[0, 0]
    lax.fori_loop(0, n_chunks, forward_body, 0)
    if not with_out:
        cfn_ref[0, 0] = c_s[...]
        nfn_ref[0, 0] = n_s[...]
        mfn_ref[0, 0] = m_s[...]


def _mlstm_scan(p, gates, m_conv, gate_bias, states, with_out):
    bsz, length, _ = p.shape
    q_blk0 = (2 * R_QK + R_WIDTH) // LANES
    k_blk0 = q_blk0 + M_QK // LANES
    v_blk0 = (2 * R_QK + R_WIDTH + 2 * M_QK) // M_DV
    c_spec = pl.BlockSpec((1, 1, M_DK, M_DV), lambda b, h: (b, h, 0, 0))
    r_spec = pl.BlockSpec((1, 1, 1, LANES), lambda b, h: (b, h, 0, 0))
    c_shape = jax.ShapeDtypeStruct((bsz, M_HEADS, M_DK, M_DV), F32)
    r_shape = jax.ShapeDtypeStruct((bsz, M_HEADS, 1, LANES), F32)
    in_specs = [pl.BlockSpec((1, length, LANES), lambda b, h: (b, 0, q_blk0 + h)),
                pl.BlockSpec((1, length, LANES), lambda b, h: (b, 0, k_blk0 + h)),
                pl.BlockSpec((1, length, M_DV), lambda b, h: (b, 0, v_blk0 + h)),
                pl.BlockSpec((1, length, LANES), lambda b, h: (b, 0, 0)),
                pl.BlockSpec((3, LANES), lambda b, h: (0, h)),
                pl.BlockSpec((3, LANES), lambda b, h: (0, M_HEADS + h)),
                pl.BlockSpec((1, LANES), lambda b, h: (0, 0)),
                c_spec, c_spec, r_spec, r_spec, r_spec, r_spec]
    scratch = [pltpu.VMEM((length, LANES), BF16), pltpu.VMEM((length, LANES), BF16),
               pltpu.VMEM((M_DK, M_DV), F32), pltpu.VMEM((1, LANES), F32), pltpu.VMEM((1, LANES), F32)]
    seq_bytes = length * (2 * LANES + M_DV) * 2 + length * LANES * 4
    if with_out:
        out_specs = pl.BlockSpec((1, length, M_DV), lambda b, h: (b, 0, h))
        out_shape = jax.ShapeDtypeStruct((bsz, length, M_WIDTH), BF16)
        scratch = scratch + [pltpu.VMEM((length, M_DV), F32)]
        vmem = 2 * seq_bytes + 2 * length * M_DV * 2 + length * M_DV * 4 + 2 * length * LANES * 2 + 6 * MIB
    else:
        out_specs = [c_spec, c_spec, r_spec, r_spec, r_spec, r_spec]
        out_shape = [c_shape, c_shape, r_shape, r_shape, r_shape, r_shape]
        vmem = 2 * seq_bytes + 2 * length * LANES * 2 + 8 * MIB
    return pl.pallas_call(
        functools.partial(_mlstm_kernel, with_out=with_out, n_chunks=length // CHUNK),
        grid=(bsz, M_HEADS),
        in_specs=in_specs,
        out_specs=out_specs,
        out_shape=out_shape,
        scratch_shapes=scratch,
        compiler_params=_params(("arbitrary", "arbitrary"), vmem),
        name="mlstm_scan" if with_out else "mlstm_state",
    )(p, p, p, gates, m_conv, m_conv, gate_bias, *states)


def _merge_kernel(x_ref, yr_ref, ym_ref, rg_ref, mo_ref, gr_ref, gm_ref, gt_ref, wr_ref, wm_ref, wo_ref, o_ref):
    rg = rg_ref[0].astype(F32)
    yr = yr_ref[0].astype(F32) * (rg * _sigmoid(rg))
    ym = ym_ref[0].astype(F32) * _sigmoid(mo_ref[0].astype(F32))
    merged = (_sigmoid(gr_ref[0].astype(F32)) * _dot(yr.astype(BF16), wr_ref[...])
              + _sigmoid(gm_ref[0].astype(F32)) * _dot(ym.astype(BF16), wm_ref[...]))
    o_ref[0] = x_ref[0] + gt_ref[0] * _dot(merged.astype(BF16), wo_ref[...])


def _merge(x, yr, ym, p, gt, wr, wm, wo):
    bsz, length, d = x.shape
    tm = 256
    rg_blk = STATE_COLS // R_WIDTH
    mo_blk = rg_blk + 1
    gr_blk = (STATE_COLS + R_WIDTH + M_WIDTH) // d
    gm_blk = gr_blk + 1
    tok = lambda width, blk: pl.BlockSpec((1, tm, width), lambda b, i: (b, i, blk))
    const = lambda shape: pl.BlockSpec(shape, lambda b, i: (0, 0))
    weights = 2 * 2 * (R_WIDTH * d + M_WIDTH * d + d * d)
    tiles = 2 * tm * (2 * d * 4 + (R_WIDTH + M_WIDTH) * 2 * 2 + 2 * d * 2)
    return pl.pallas_call(
        _merge_kernel,
        grid=(bsz, length // tm),
        in_specs=[tok(d, 0), tok(R_WIDTH, 0), tok(M_WIDTH, 0), tok(R_WIDTH, rg_blk), tok(M_WIDTH, mo_blk),
                  tok(d, gr_blk), tok(d, gm_blk),
                  pl.BlockSpec((1, 1, d), lambda b, i: (b, 0, 0)),
                  const((R_WIDTH, d)), const((M_WIDTH, d)), const((d, d))],
        out_specs=tok(d, 0),
        out_shape=jax.ShapeDtypeStruct((bsz, length, d), F32),
        compiler_params=_params(("arbitrary", "arbitrary"), weights + tiles + 12 * MIB),
        name="merge",
    )(x, yr, ym, p, p, p, p, gt, wr, wm, wo)


def _peer_select_kernel(x_ref, g_ref, sh_ref, sc_ref, wq_ref, keys_ref, h_ref, sel_ref, score_s, work_s, top_s,
                        cand_s):
    tm = x_ref.shape[1]
    neg_inf = jnp.float32(-jnp.inf)
    x = x_ref[0]
    y = x * lax.rsqrt(jnp.mean(x * x, axis=-1, keepdims=True) + EPS) * g_ref[...]
    hb = (y * (1.0 + sc_ref[0]) + sh_ref[0]).astype(BF16)
    h_ref[0] = hb
    query = _dot(hb, wq_ref[...]).astype(BF16)
    for hp in range(2 * P_HEADS):
        score_s[hp] = _dot_nt(keys_ref[hp], query[:, hp * P_NKEYS:(hp + 1) * P_NKEYS])

    key_id = lax.broadcasted_iota(jnp.int32, (P_NKEYS, tm), 0)
    cand_id = lax.broadcasted_iota(jnp.int32, (P_TOPK * P_TOPK, tm), 0)
    stat_id = lax.broadcasted_iota(jnp.int32, (SEL_ROWS - 2 * P_NKEYS, tm), 0)

    def extract_max(ref, ids, sentinel):
        w = ref[...]
        m = jnp.max(w, axis=0, keepdims=True)
        first = jnp.min(jnp.where(w == m, ids, sentinel), axis=0, keepdims=True)
        ref[...] = jnp.where(ids == first, neg_inf, w)
        return m

    def subkey_body(hp, carry):
        scores = score_s[hp]
        work_s[...] = scores

        def round_body(r, c2):
            top_s[hp, pl.ds(r, 1), :] = extract_max(work_s, key_id, P_NKEYS)
            return c2

        lax.fori_loop(0, P_TOPK, round_body, 0)
        sel_ref[0, hp // 2, pl.ds(pl.multiple_of((hp % 2) * P_NKEYS, P_NKEYS), P_NKEYS), :] = jnp.where(
            work_s[...] < scores, scores, neg_inf)
        return carry

    lax.fori_loop(0, 2 * P_HEADS, subkey_body, 0)

    def head_body(h, carry):
        top1 = top_s[2 * h]
        top2 = top_s[2 * h + 1]
        cand_s[...] = jnp.concatenate([top1[a:a + 1] + top2 for a in range(P_TOPK)], axis=0)
        best = extract_max(cand_s, cand_id, P_TOPK * P_TOPK)

        def round_body(r, c2):
            m = extract_max(cand_s, cand_id, P_TOPK * P_TOPK)
            return m, c2[1] + jnp.exp(m - best)

        tau, z = lax.fori_loop(1, P_TOPK, round_body, (best, jnp.ones_like(best)))
        sel_ref[0, h, 2 * P_NKEYS:SEL_ROWS, :] = jnp.where(
            stat_id == 0, tau, jnp.where(stat_id == 1, top1[0:1], jnp.where(stat_id == 2, top2[0:1],
                                                                            jnp.where(stat_id == 3, 1.0 / z, 0.0))))
        return carry

    lax.fori_loop(0, P_HEADS, head_body, 0)


def _peer_select(x1, g, sh, sc, wq, keys, tm):
    bsz, length, d = x1.shape
    tiles = length // tm
    vmem = (2 * tm * d * 4 + 2 * d * d * 2 + 2 * tm * d * 2 + 2 * P_HEADS * SEL_ROWS * tm * 4
            + (2 * P_HEADS + 1) * P_NKEYS * tm * 4 + 3 * tm * d * 4 + 8 * MIB)
    return pl.pallas_call(
        _peer_select_kernel,
        grid=(bsz, tiles),
        in_specs=[pl.BlockSpec((1, tm, d), lambda b, i: (b, i, 0)),
                  pl.BlockSpec((1, d), lambda b, i: (0, 0)),
                  pl.BlockSpec((1, 1, d), lambda b, i: (b, 0, 0)),
                  pl.BlockSpec((1, 1, d), lambda b, i: (b, 0, 0)),
                  pl.BlockSpec((d, d), lambda b, i: (0, 0)),
                  pl.BlockSpec((2 * P_HEADS, P_NKEYS, P_NKEYS), lambda b, i: (0, 0, 0))],
        out_specs=[pl.BlockSpec((1, tm, d), lambda b, i: (b, i, 0)),
                   pl.BlockSpec((1, P_HEADS, SEL_ROWS, tm), lambda b, i: (b * tiles + i, 0, 0, 0))],
        out_shape=[jax.ShapeDtypeStruct((bsz, length, d), BF16),
                   jax.ShapeDtypeStruct((bsz * tiles, P_HEADS, SEL_ROWS, tm), F32)],
        scratch_shapes=[pltpu.VMEM((2 * P_HEADS, P_NKEYS, tm), F32),
                        pltpu.VMEM((P_NKEYS, tm), F32),
                        pltpu.VMEM((2 * P_HEADS, P_TOPK, tm), F32),
                        pltpu.VMEM((P_TOPK * P_TOPK, tm), F32)],
        compiler_params=_params(("arbitrary", "arbitrary"), vmem),
        name="peer_select",
    )(x1, g, sh, sc, wq, keys)


def _peer_expert_kernel(h_ref, sel_ref, down_ref, up_ref, x_ref, gt_ref, fg_ref, o_ref, acc, e1_s, e2_s, *,
                        n_steps):
    j = pl.program_id(2)
    chunk_experts = down_ref.shape[0]
    tm = h_ref.shape[1]

    @pl.when(j == 0)
    def _():
        acc[...] = jnp.zeros_like(acc)
        for h in range(P_HEADS):
            stats = sel_ref[0, h, 2 * P_NKEYS:SEL_ROWS, :]
            e1_s[h] = jnp.exp(sel_ref[0, h, 0:P_NKEYS, :] - stats[1:2])
            e2_s[h] = jnp.exp(sel_ref[0, h, P_NKEYS:2 * P_NKEYS, :] - stats[2:3]) * stats[3:4]

    a = _dot_nt(down_ref[...], h_ref[0])
    act = 0.5 * a * (1.0 + lax.erf(a * (2.0 ** -0.5)))
    pieces = []
    for kk in range(chunk_experts // P_NKEYS):
        k1 = j * (chunk_experts // P_NKEYS) + kk
        w = jnp.zeros((P_NKEYS, tm), F32)
        for h in range(P_HEADS):
            total = sel_ref[0, h, P_NKEYS:2 * P_NKEYS, :] + sel_ref[0, h, pl.ds(k1, 1), :]
            tau = sel_ref[0, h, 2 * P_NKEYS:2 * P_NKEYS + 1, :]
            w = w + jnp.where(total >= tau, e2_s[h] * e1_s[h, pl.ds(k1, 1), :], 0.0)
        pieces.append(act[kk * P_NKEYS:(kk + 1) * P_NKEYS] * w)
    coef = jnp.concatenate(pieces, axis=0)
    acc[...] += _dot(coef.T.astype(BF16), up_ref[...])

    @pl.when(j == n_steps - 1)
    def _():
        x2 = x_ref[0] + gt_ref[0] * acc[...]
        o_ref[0] = x2 * lax.rsqrt(jnp.mean(x2 * x2, axis=-1, keepdims=True) + EPS) * fg_ref[...]


def _peer_expert(h2, sel, down, up, x1, gt, final_g, tm):
    bsz, length, d = x1.shape
    tiles = length // tm
    ec = 512
    n_steps = P_EXPERTS // ec
    vmem = (2 * tm * d * 2 + 2 * P_HEADS * SEL_ROWS * tm * 4 + 4 * ec * d * 2 + 4 * tm * d * 4 + tm * d * 4
            + 2 * P_HEADS * P_NKEYS * tm * 4 + 4 * ec * tm * 4 + 6 * MIB)
    return pl.pallas_call(
        functools.partial(_peer_expert_kernel, n_steps=n_steps),
        grid=(bsz, tiles, n_steps),
        in_specs=[pl.BlockSpec((1, tm, d), lambda b, i, j: (b, i, 0)),
                  pl.BlockSpec((1, P_HEADS, SEL_ROWS, tm), lambda b, i, j: (b * tiles + i, 0, 0, 0)),
                  pl.BlockSpec((ec, d), lambda b, i, j: (j, 0)),
                  pl.BlockSpec((ec, d), lambda b, i, j: (j, 0)),
                  pl.BlockSpec((1, tm, d), lambda b, i, j: (b, i, 0)),
                  pl.BlockSpec((1, 1, d), lambda b, i, j: (b, 0, 0)),
                  pl.BlockSpec((1, d), lambda b, i, j: (0, 0))],
        out_specs=pl.BlockSpec((1, tm, d), lambda b, i, j: (b, i, 0)),
        out_shape=jax.ShapeDtypeStruct((bsz, length, d), F32),
        scratch_shapes=[pltpu.VMEM((tm, d), F32),
                        pltpu.VMEM((P_HEADS, P_NKEYS, tm), F32),
                        pltpu.VMEM((P_HEADS, P_NKEYS, tm), F32)],
        compiler_params=_params(("arbitrary", "arbitrary", "arbitrary"), vmem),
        name="peer_expert",
    )(h2, sel, down, up, x1, gt, final_g)


def _rope_tables(length):
    rows = length // GRID_W
    row = jnp.repeat(jnp.arange(rows, dtype=F32), GRID_W)
    col = jnp.tile(jnp.arange(GRID_W, dtype=F32), rows)
    n_ax = R_DK // 4
    freq = ROPE_BASE ** (-jnp.arange(n_ax, dtype=F32) / n_ax)
    ang = jnp.concatenate([row[:, None] * freq, col[:, None] * freq], axis=-1)
    cos = jnp.tile(jnp.cos(ang), (1, LANES // (R_DK // 2)))
    sin = jnp.sin(ang)
    sin_signed = jnp.tile(jnp.concatenate([-sin, sin], axis=-1), (1, LANES // R_DK))
    return cos, sin_signed


def _layer(x, ctx, c, c_ctx, w_ada, b_ada, norm1_g, w_in, ret_decay, m_conv, m_gate_bias, w_ret_out, w_mlstm_out,
           w_out, norm2_g, peer_query, peer_keys, peer_down, peer_up, final_g):
    bsz, length, d = x.shape

    c_rows = jnp.zeros((8, d), F32).at[:bsz].set(c).at[bsz].set(c_ctx)
    mod = _ada(c_rows, w_ada, b_ada)
    sh1, sc1, gt1, sh2, sc2, gt2 = [m[:bsz, None, :] for m in jnp.split(mod, ADA_CHUNKS, axis=-1)]
    csh1, csc1 = [jnp.broadcast_to(m[bsz][None, None, :], (bsz, 1, d))
                  for m in jnp.split(mod, ADA_CHUNKS, axis=-1)[:2]]

    o_rq, o_rk, o_rv, o_rg = 0, R_QK, 2 * R_QK, 2 * R_QK + R_WIDTH
    o_mq = o_rg + R_WIDTH
    o_mk, o_mv, o_mo = o_mq + M_QK, o_mq + 2 * M_QK, o_mq + 2 * M_QK + M_WIDTH
    o_mg = o_mo + M_WIDTH
    o_gr = o_mg + M_N_GATES
    wb = w_in.astype(BF16)
    w_p = jnp.concatenate([wb[:, o_rq:o_rg], wb[:, o_mq:o_mo], wb[:, o_rg:o_mq], wb[:, o_mo:o_mg], wb[:, o_gr:]],
                          axis=1)
    w_g = jnp.pad(wb[:, o_mg:o_gr], ((0, 0), (0, LANES - M_N_GATES)))
    gate_bias = jnp.pad(m_gate_bias.reshape(1, M_N_GATES), ((0, 0), (0, LANES - M_N_GATES)))
    g1 = norm1_g.reshape(1, d)

    p_ctx, gates_ctx = _inproj(ctx, g1, csh1, csc1, w_p, w_g, STATE_COLS, None)
    p_lat, gates_lat = _inproj(x, g1, sh1, sc1, w_p, w_g, w_p.shape[1], _rope_tables(length))

    zero_s = jnp.zeros((bsz, R_HEADS // 2, 2 * R_DK, 2 * R_DV), F32)
    sf0, sb0 = _ret_scan(ret_decay, p_ctx, zero_s, zero_s, with_out=False)
    yr = _ret_scan(ret_decay, p_lat, sf0, sb0, with_out=True)

    zero_c = jnp.zeros((bsz, M_HEADS, M_DK, M_DV), F32)
    zero_r = jnp.zeros((bsz, M_HEADS, 1, LANES), F32)
    m_states = _mlstm_scan(p_ctx, gates_ctx, m_conv, gate_bias,
                           (zero_c, zero_c, zero_r, zero_r, zero_r, zero_r), with_out=False)
    ym = _mlstm_scan(p_lat, gates_lat, m_conv, gate_bias, m_states, with_out=True)

    x1 = _merge(x, yr, ym, p_lat, gt1, w_ret_out.astype(BF16), w_mlstm_out.astype(BF16), w_out.astype(BF16))

    tm = 512
    keys = peer_keys.reshape(2 * P_HEADS, P_NKEYS, P_NKEYS).astype(BF16)
    h2, sel = _peer_select(x1, norm2_g.reshape(1, d), sh2, sc2, peer_query.astype(BF16), keys, tm)
    return _peer_expert(h2, sel, peer_down.astype(BF16), peer_up.astype(BF16), x1, gt2, final_g.reshape(1, d), tm)


def kernel(x, c, ctx, c_ctx, w_ada, b_ada, norm1_g, w_in, ret_decay, m_conv, m_gate_bias, w_ret_out, w_mlstm_out,
           w_out, norm2_g, peer_query, peer_keys, peer_down, peer_up, final_g):
    assert w_ada.shape[0] == 1, "single-layer block: the context stream is read, never updated"
    return _layer(x, ctx, c, c_ctx, w_ada[0], b_ada[0], norm1_g[0], w_in[0], ret_decay[0], m_conv[0],
                  m_gate_bias[0], w_ret_out[0], w_mlstm_out[0], w_out[0], norm2_g[0], peer_query[0], peer_keys[0],
                  peer_down[0], peer_up[0], final_g)
```

```python
import functools

import jax
import jax.numpy as jnp
from jax import lax
from jax.experimental import pallas as pl
from jax.experimental.pallas import tpu as pltpu

F32 = jnp.float32
BF16 = jnp.bfloat16

EPS = 1e-6
CHUNK = 128
GRID_W = 64
ROPE_BASE = 10000.0
ADA_CHUNKS = 6

R_HEADS, R_DK, R_DV = 8, 64, 128
M_HEADS, M_DK, M_DV = 4, 128, 256
R_QK, R_WIDTH = R_HEADS * R_DK, R_HEADS * R_DV
M_QK, M_WIDTH = M_HEADS * M_DK, M_HEADS * M_DV
M_N_GATES = 2 * 2 * M_HEADS

P_HEADS, P_NKEYS, P_TOPK = 8, 128, 16
P_EXPERTS = P_NKEYS * P_NKEYS

LANES = 128
MIB = 1024 * 1024
VMEM_BUDGET_BYTES = 56 * MIB

STATE_COLS = 2 * R_QK + R_WIDTH + 2 * M_QK + M_WIDTH
SEL_S2, SEL_TH, SEL_E1, SEL_E2 = 0, P_NKEYS, 2 * P_NKEYS, 3 * P_NKEYS
SEL_ROWS = 4 * P_NKEYS


def _params(semantics, vmem_bytes, flags=None):
    return pltpu.CompilerParams(dimension_semantics=semantics,
                                vmem_limit_bytes=int(min(VMEM_BUDGET_BYTES, vmem_bytes)), flags=flags)


def _log_sigmoid(x):
    return jnp.minimum(x, 0.0) - jnp.log1p(jnp.exp(-jnp.abs(x)))


def _sigmoid(x):
    return 1.0 / (1.0 + jnp.exp(-x))


def _dot(a, b):
    return jnp.dot(a, b, preferred_element_type=F32)


def _dot_nt(a, b):
    return lax.dot_general(a, b, (((1,), (1,)), ((), ())), preferred_element_type=F32)


def _group_norm(y, group):
    outs = []
    for g in range(y.shape[1] // group):
        yg = y[:, g * group:(g + 1) * group]
        yc = yg - jnp.mean(yg, axis=-1, keepdims=True)
        outs.append(yc * lax.rsqrt(jnp.mean(yc * yc, axis=-1, keepdims=True) + EPS))
    return jnp.concatenate(outs, axis=1)


def _ada_kernel(c_ref, w_ref, b_ref, o_ref):
    c = c_ref[...]
    a = c * _sigmoid(c)
    o_ref[...] = jnp.dot(a, w_ref[...], precision=lax.Precision.HIGHEST, preferred_element_type=F32) + b_ref[...]


def _ada(c_rows, w_ada, b_ada):
    rows, d = c_rows.shape
    n = w_ada.shape[1]
    tn = 1024
    return pl.pallas_call(
        _ada_kernel,
        grid=(n // tn,),
        in_specs=[pl.BlockSpec((rows, d), lambda j: (0, 0)),
                  pl.BlockSpec((d, tn), lambda j: (0, j)),
                  pl.BlockSpec((1, tn), lambda j: (0, j))],
        out_specs=pl.BlockSpec((rows, tn), lambda j: (0, j)),
        out_shape=jax.ShapeDtypeStruct((rows, n), F32),
        compiler_params=_params(("arbitrary",), 2 * d * tn * 4 + 8 * MIB),
        name="ada",
    )(c_rows, w_ada, b_ada.reshape(1, n))


def _rope(t, cos, sin_signed):
    lane = lax.broadcasted_iota(jnp.int32, (1, LANES), 1)
    first_half = jnp.bitwise_and(lane, R_DK - 1) < (R_DK // 2)
    outs = []
    for g in range(t.shape[1] // LANES):
        tg = t[:, g * LANES:(g + 1) * LANES]
        partner = jnp.where(first_half,
                            pltpu.roll(tg, LANES - R_DK // 2, axis=1),
                            pltpu.roll(tg, R_DK // 2, axis=1))
        outs.append(tg * cos + partner * sin_signed)
    return jnp.concatenate(outs, axis=1)


def _inproj_kernel(x_ref, g_ref, sh_ref, sc_ref, w_ref, wg_ref, *rest, rope):
    if rope:
        cos_ref, sin_ref, p_ref, gate_ref, xn_ref = rest
    else:
        p_ref, gate_ref, xn_ref = rest
    j = pl.program_id(2)

    @pl.when(j == 0)
    def _():
        x = x_ref[0]
        y = x * lax.rsqrt(jnp.mean(x * x, axis=-1, keepdims=True) + EPS) * g_ref[...]
        xn = (y * (1.0 + sc_ref[0]) + sh_ref[0]).astype(BF16)
        xn_ref[...] = xn
        gate_ref[0] = _dot(xn, wg_ref[...])

    acc = _dot(xn_ref[...], w_ref[...])
    if rope:
        @pl.when(j == 0)
        def _():
            p_ref[0] = _rope(acc, cos_ref[...], sin_ref[...]).astype(BF16)

        @pl.when(j != 0)
        def _():
            p_ref[0] = acc.astype(BF16)
    else:
        p_ref[0] = acc.astype(BF16)


def _inproj(x, g, sh, sc, w, wg, ncols, rope_tabs):
    bsz, length, d = x.shape
    tm = min(1024, length)
    tn = 2 * R_QK
    rope = rope_tabs is not None
    in_specs = [pl.BlockSpec((1, tm, d), lambda b, i, j: (b, i, 0)),
                pl.BlockSpec((1, d), lambda b, i, j: (0, 0)),
                pl.BlockSpec((1, 1, d), lambda b, i, j: (b, 0, 0)),
                pl.BlockSpec((1, 1, d), lambda b, i, j: (b, 0, 0)),
                pl.BlockSpec((d, tn), lambda b, i, j: (0, j)),
                pl.BlockSpec((d, LANES), lambda b, i, j: (0, 0))]
    args = [x, g, sh, sc, w, wg]
    if rope:
        in_specs += [pl.BlockSpec((tm, LANES), lambda b, i, j: (i, 0))] * 2
        args += list(rope_tabs)
    vmem = 2 * tm * d * 4 + 2 * d * tn * 2 + 2 * tm * tn * 2 + tm * d * 2 + 3 * tm * tn * 4 + 8 * MIB
    return pl.pallas_call(
        functools.partial(_inproj_kernel, rope=rope),
        grid=(bsz, length // tm, ncols // tn),
        in_specs=in_specs,
        out_specs=[pl.BlockSpec((1, tm, tn), lambda b, i, j: (b, i, j)),
                   pl.BlockSpec((1, tm, LANES), lambda b, i, j: (b, i, 0))],
        out_shape=[jax.ShapeDtypeStruct((bsz, length, ncols), BF16),
                   jax.ShapeDtypeStruct((bsz, length, LANES), F32)],
        scratch_shapes=[pltpu.VMEM((tm, d), BF16)],
        compiler_params=_params(("arbitrary", "arbitrary", "arbitrary"), vmem),
        name="inproj_rope" if rope else "inproj_ctx",
    )(*args)


def _ret_kernel(dec_ref, q_ref, k_ref, v_ref, sf0_ref, sb0_ref, *rest, with_out, n_chunks):
    if with_out:
        y_ref, sf, sb, yacc = rest
    else:
        sfn_ref, sbn_ref, sf, sb = rest
    pair = pl.program_id(1)
    head_a = 2 * pair
    head_b = head_a + 1
    lane = lax.broadcasted_iota(jnp.int32, (1, LANES), 1)
    lane_v = lax.broadcasted_iota(jnp.int32, (1, 2 * R_DV), 1)
    row = lax.broadcasted_iota(jnp.int32, (CHUNK, 1), 0)
    pos = row.astype(F32)
    in_a = lane < R_DK
    block_diag = (row < R_DK) == (lane_v < R_DV)

    def decay_lanes(d):
        return _log_sigmoid(jnp.where(in_a, dec_ref[d, head_a], dec_ref[d, head_b]))

    def decay_rows(d):
        return _log_sigmoid(jnp.where(row < R_DK, dec_ref[d, head_a], dec_ref[d, head_b]))

    lgf, lgb = decay_lanes(0), decay_lanes(1)
    scale = R_DK ** -0.5
    qdec_f = jnp.exp(lgf * (pos + 1.0))
    kdec_f = jnp.exp(lgf * (CHUNK - 1.0 - pos)) * scale
    qdec_b = jnp.exp(lgb * (CHUNK - pos))
    kdec_b = jnp.exp(lgb * pos) * scale
    cdec_f = jnp.exp(decay_rows(0) * CHUNK)
    cdec_b = jnp.exp(decay_rows(1) * CHUNK)

    sf[...] = sf0_ref[0, 0]
    sb[...] = sb0_ref[0, 0]

    def load(c):
        s = pl.multiple_of(c * CHUNK, CHUNK)
        return s, q_ref[0, pl.ds(s, CHUNK), :], k_ref[0, pl.ds(s, CHUNK), :], v_ref[0, pl.ds(s, CHUNK), :]

    def state_update(state_ref, k, v, kdec, cdec):
        kt = (k.astype(F32) * kdec).T.astype(BF16)
        state_ref[...] = cdec * state_ref[...] + jnp.where(block_diag, _dot(kt, v), 0.0)

    def backward_body(i, carry):
        s, q, k, v = load(n_chunks - 1 - i)
        if with_out:
            yacc[pl.ds(s, CHUNK), :] = _dot((q.astype(F32) * qdec_b).astype(BF16), sb[...].astype(BF16))
        state_update(sb, k, v, kdec_b, cdec_b)
        return carry

    lax.fori_loop(0, n_chunks, backward_body, 0)

    if with_out:
        rel = (row - lane).astype(F32)

        def intra_decay(head):
            lf = _log_sigmoid(jnp.full((1, LANES), dec_ref[0, head], F32))
            lb = _log_sigmoid(jnp.full((1, LANES), dec_ref[1, head], F32))
            fwd = jnp.where(rel >= 0, jnp.exp(lf * jnp.maximum(rel, 0.0)), 0.0)
            bwd = jnp.where(rel <= 0, jnp.exp(lb * jnp.maximum(-rel, 0.0)), 0.0)
            return (fwd + bwd) * scale

        d_a = intra_decay(head_a)
        d_b = intra_decay(head_b)

    def forward_body(c, carry):
        s, q, k, v = load(c)
        if with_out:
            qf = q.astype(F32)
            q_a = jnp.where(in_a, qf, 0.0).astype(BF16)
            q_b = jnp.where(in_a, 0.0, qf).astype(BF16)
            att_a = (_dot_nt(q_a, k) * d_a).astype(BF16)
            att_b = (_dot_nt(q_b, k) * d_b).astype(BF16)
            y = jnp.concatenate([_dot(att_a, v[:, :R_DV]), _dot(att_b, v[:, R_DV:])], axis=1)
            y = y + _dot((qf * qdec_f).astype(BF16), sf[...].astype(BF16)) + yacc[pl.ds(s, CHUNK), :]
            y_ref[0, pl.ds(s, CHUNK), :] = _group_norm(y, R_DV).astype(BF16)
        state_update(sf, k, v, kdec_f, cdec_f)
        return carry

    lax.fori_loop(0, n_chunks, forward_body, 0)

    if not with_out:
        sfn_ref[0, 0] = sf[...]
        sbn_ref[0, 0] = sb[...]


def _ret_scan(ret_decay, p, sf0, sb0, with_out):
    bsz, length, _ = p.shape
    n_pairs = R_HEADS // 2
    k_blk0 = R_QK // LANES
    v_blk0 = 2 * R_QK // (2 * R_DV)
    state_spec = pl.BlockSpec((1, 1, 2 * R_DK, 2 * R_DV), lambda b, h: (b, h, 0, 0))
    state_shape = jax.ShapeDtypeStruct((bsz, n_pairs, 2 * R_DK, 2 * R_DV), F32)
    in_specs = [pl.BlockSpec(memory_space=pltpu.SMEM),
                pl.BlockSpec((1, length, LANES), lambda b, h: (b, 0, h)),
                pl.BlockSpec((1, length, LANES), lambda b, h: (b, 0, k_blk0 + h)),
                pl.BlockSpec((1, length, 2 * R_DV), lambda b, h: (b, 0, v_blk0 + h)),
                state_spec, state_spec]
    scratch = [pltpu.VMEM((2 * R_DK, 2 * R_DV), F32)] * 2
    seq_bytes = length * (2 * LANES + 2 * R_DV) * 2
    if with_out:
        out_specs = pl.BlockSpec((1, length, 2 * R_DV), lambda b, h: (b, 0, h))
        out_shape = jax.ShapeDtypeStruct((bsz, length, R_WIDTH), BF16)
        scratch = scratch + [pltpu.VMEM((length, 2 * R_DV), F32)]
        vmem = 2 * seq_bytes + 2 * length * 2 * R_DV * 2 + length * 2 * R_DV * 4 + 8 * MIB
    else:
        out_specs = [state_spec, state_spec]
        out_shape = [state_shape, state_shape]
        vmem = 2 * seq_bytes + 8 * MIB
    return pl.pallas_call(
        functools.partial(_ret_kernel, with_out=with_out, n_chunks=length // CHUNK),
        grid=(bsz, n_pairs),
        in_specs=in_specs,
        out_specs=out_specs,
        out_shape=out_shape,
        scratch_shapes=scratch,
        compiler_params=_params(("arbitrary", "arbitrary"), vmem),
        name="ret_scan" if with_out else "ret_state",
    )(ret_decay, p, p, p, sf0, sb0)


HALO = 16


def _mlstm_kernel(q_ref, k_ref, v_ref, g_ref, cwq_ref, cwk_ref, gb_ref,
                  c0f_ref, c0b_ref, n0f_ref, n0b_ref, m0f_ref, m0b_ref, *rest, with_out, n_chunks):
    if with_out:
        y_ref, qc, kc, c_s, n_s, m_s, yacc = rest
    else:
        cfn_ref, cbn_ref, nfn_ref, nbn_ref, mfn_ref, mbn_ref, qc, kc, c_s, n_s, m_s = rest
    head = pl.program_id(1)
    length = n_chunks * CHUNK
    lane = lax.broadcasted_iota(jnp.int32, (1, LANES), 1)
    row = lax.broadcasted_iota(jnp.int32, (CHUNK, 1), 0)
    neg_inf = jnp.float32(-jnp.inf)

    def conv_body(c, carry):
        s = pl.multiple_of(c * CHUNK, CHUNK)
        sp = pl.multiple_of(jnp.maximum(s - HALO, 0), HALO)
        sn = pl.multiple_of(jnp.minimum(s + CHUNK, length - HALO), HALO)
        has_prev = (c > 0).astype(F32)
        has_next = (c < n_chunks - 1).astype(F32)
        for src, w_ref, dst, out_scale in ((q_ref, cwq_ref, qc, 1.0), (k_ref, cwk_ref, kc, M_DK ** -0.5)):
            xw = jnp.concatenate([src[0, pl.ds(sp, HALO), :].astype(F32) * has_prev,
                                  src[0, pl.ds(s, CHUNK), :].astype(F32),
                                  src[0, pl.ds(sn, HALO), :].astype(F32) * has_next], axis=0)
            rows = CHUNK + 2 * HALO
            w = w_ref[...]
            y = (w[0:1] * pltpu.roll(xw, 1, axis=0) + w[1:2] * xw + w[2:3] * pltpu.roll(xw, rows - 1, axis=0))
            y = y[HALO:HALO + CHUNK]
            dst[pl.ds(s, CHUNK), :] = (y * _sigmoid(y) * out_scale).astype(BF16)
        return carry

    lax.fori_loop(0, n_chunks, conv_body, 0)

    def column(mat, idx):
        return jnp.sum(jnp.where(lane == idx, mat, 0.0), axis=-1, keepdims=True)

    def step(c, forward):
        s = pl.multiple_of(c * CHUNK, CHUNK)
        q = qc[pl.ds(s, CHUNK), :]
        k = kc[pl.ds(s, CHUNK), :]
        v = v_ref[0, pl.ds(s, CHUNK), :]
        gates = g_ref[0, pl.ds(s, CHUNK), :] + gb_ref[...]
        logf = _log_sigmoid(gates)
        csum = logf
        for sh in (1, 2, 4, 8, 16, 32, 64):
            csum = csum + jnp.where(row >= sh, pltpu.roll(csum, sh, axis=0), 0.0)
        if forward:
            b_col = column(csum, M_HEADS + head)
            i_col = column(gates, head)
            end_row = CHUNK - 1
            keep = row >= lane
        else:
            total = csum[CHUNK - 1:CHUNK, :]
            b_col = column(total - csum + logf, 3 * M_HEADS + head)
            i_col = column(gates, 2 * M_HEADS + head)
            end_row = 0
            keep = lane >= row
        b_mat = jnp.broadcast_to(b_col, (CHUNK, LANES))
        u_col = i_col - b_col
        u_row = jnp.where(lane == 0, b_col, jnp.where(lane == 1, u_col, 0.0)).T[1:2, :]
        b_end = jnp.sum(jnp.where(row == end_row, b_mat, 0.0), axis=0, keepdims=True)

        c_prev = c_s[...]
        n_prev = n_s[...]
        m_prev = m_s[...]
        w_log = b_end + u_col
        a_end = b_end + m_prev
        m_new = jnp.maximum(a_end, jnp.max(w_log, axis=0, keepdims=True))
        decay = jnp.exp(a_end - m_new)
        kw = k.astype(F32) * jnp.exp(w_log - m_new)
        c_s[...] = jnp.concatenate([decay, decay], axis=1) * c_prev + _dot(kw.T.astype(BF16), v)
        n_s[...] = decay * n_prev + jnp.sum(kw, axis=0, keepdims=True)
        m_s[...] = m_new
        if not with_out:
            return None
        a_mat = b_mat + m_prev
        d_log = jnp.where(keep, b_col + u_row, neg_inf)
        m_t = jnp.max(jnp.maximum(d_log, a_mat), axis=-1, keepdims=True)
        att = _dot_nt(q, k) * jnp.exp(d_log - m_t)
        inter = jnp.exp(jnp.max(a_mat, axis=-1, keepdims=True) - m_t)
        num = _dot(att.astype(BF16), v) + inter * _dot(q, c_prev.astype(BF16))
        den = (jnp.sum(att, axis=-1, keepdims=True)
               + inter * jnp.sum(q.astype(F32) * n_prev, axis=-1, keepdims=True))
        return s, num / jnp.maximum(jnp.abs(den), jnp.exp(-m_t))

    def backward_body(i, carry):
        out = step(n_chunks - 1 - i, forward=False)
        if with_out:
            yacc[pl.ds(out[0], CHUNK), :] = out[1]
        return carry

    def forward_body(c, carry):
        out = step(c, forward=True)
        if with_out:
            s, h = out
            y_ref[0, pl.ds(s, CHUNK), :] = _group_norm(h + yacc[pl.ds(s, CHUNK), :], M_DV).astype(BF16)
        return carry

    c_s[...] = c0b_ref[0, 0]
    n_s[...] = n0b_ref[0, 0]
    m_s[...] = m0b_ref[0, 0]
    lax.fori_loop(0, n_chunks, backward_body, 0)
    if not with_out:
        cbn_ref[0, 0] = c_s[...]
        nbn_ref[0, 0] = n_s[...]
        mbn_ref[0, 0] = m_s[...]

    c_s[...] = c0f_ref[0, 0]
    n_s[...] = n0f_ref[0, 0]
    m_s[...] = m0f_ref[0, 0]
    lax.fori_loop(0, n_chunks, forward_body, 0)
    if not with_out:
        cfn_ref[0, 0] = c_s[...]
        nfn_ref[0, 0] = n_s[...]
        mfn_ref[0, 0] = m_s[...]


def _mlstm_scan(p, gates, m_conv, gate_bias, states, with_out):
    bsz, length, _ = p.shape
    q_blk0 = (2 * R_QK + R_WIDTH) // LANES
    k_blk0 = q_blk0 + M_QK // LANES
    v_blk0 = (2 * R_QK + R_WIDTH + 2 * M_QK) // M_DV
    c_spec = pl.BlockSpec((1, 1, M_DK, M_DV), lambda b, h: (b, h, 0, 0))
    r_spec = pl.BlockSpec((1, 1, 1, LANES), lambda b, h: (b, h, 0, 0))
    c_shape = jax.ShapeDtypeStruct((bsz, M_HEADS, M_DK, M_DV), F32)
    r_shape = jax.ShapeDtypeStruct((bsz, M_HEADS, 1, LANES), F32)
    in_specs = [pl.BlockSpec((1, length, LANES), lambda b, h: (b, 0, q_blk0 + h)),
                pl.BlockSpec((1, length, LANES), lambda b, h: (b, 0, k_blk0 + h)),
                pl.BlockSpec((1, length, M_DV), lambda b, h: (b, 0, v_blk0 + h)),
                pl.BlockSpec((1, length, LANES), lambda b, h: (b, 0, 0)),
                pl.BlockSpec((3, LANES), lambda b, h: (0, h)),
                pl.BlockSpec((3, LANES), lambda b, h: (0, M_HEADS + h)),
                pl.BlockSpec((1, LANES), lambda b, h: (0, 0)),
                c_spec, c_spec, r_spec, r_spec, r_spec, r_spec]
    scratch = [pltpu.VMEM((length, LANES), BF16), pltpu.VMEM((length, LANES), BF16),
               pltpu.VMEM((M_DK, M_DV), F32), pltpu.VMEM((1, LANES), F32), pltpu.VMEM((1, LANES), F32)]
    seq_bytes = length * (2 * LANES + M_DV) * 2 + length * LANES * 4
    if with_out:
        out_specs = pl.BlockSpec((1, length, M_DV), lambda b, h: (b, 0, h))
        out_shape = jax.ShapeDtypeStruct((bsz, length, M_WIDTH), BF16)
        scratch = scratch + [pltpu.VMEM((length, M_DV), F32)]
        vmem = 2 * seq_bytes + 2 * length * M_DV * 2 + length * M_DV * 4 + 2 * length * LANES * 2 + 6 * MIB
    else:
        out_specs = [c_spec, c_spec, r_spec, r_spec, r_spec, r_spec]
        out_shape = [c_shape, c_shape, r_shape, r_shape, r_shape, r_shape]
        vmem = 2 * seq_bytes + 2 * length * LANES * 2 + 8 * MIB
    return pl.pallas_call(
        functools.partial(_mlstm_kernel, with_out=with_out, n_chunks=length // CHUNK),
        grid=(bsz, M_HEADS),
        in_specs=in_specs,
        out_specs=out_specs,
        out_shape=out_shape,
        scratch_shapes=scratch,
        compiler_params=_params(("arbitrary", "arbitrary"), vmem),
        name="mlstm_scan" if with_out else "mlstm_state",
    )(p, p, p, gates, m_conv, m_conv, gate_bias, *states)


def _merge_kernel(x_ref, yr_ref, ym_ref, rg_ref, mo_ref, gr_ref, gm_ref, gt_ref, wr_ref, wm_ref, wo_ref, o_ref):
    rg = rg_ref[0].astype(F32)
    yr = yr_ref[0].astype(F32) * (rg * _sigmoid(rg))
    ym = ym_ref[0].astype(F32) * _sigmoid(mo_ref[0].astype(F32))
    merged = (_sigmoid(gr_ref[0].astype(F32)) * _dot(yr.astype(BF16), wr_ref[...])
              + _sigmoid(gm_ref[0].astype(F32)) * _dot(ym.astype(BF16), wm_ref[...]))
    o_ref[0] = x_ref[0] + gt_ref[0] * _dot(merged.astype(BF16), wo_ref[...])


def _merge(x, yr, ym, p, gt, wr, wm, wo):
    bsz, length, d = x.shape
    tm = 256
    rg_blk = STATE_COLS // R_WIDTH
    mo_blk = rg_blk + 1
    gr_blk = (STATE_COLS + R_WIDTH + M_WIDTH) // d
    gm_blk = gr_blk + 1
    tok = lambda width, blk: pl.BlockSpec((1, tm, width), lambda b, i: (b, i, blk))
    const = lambda shape: pl.BlockSpec(shape, lambda b, i: (0, 0))
    weights = 2 * 2 * (R_WIDTH * d + M_WIDTH * d + d * d)
    tiles = 2 * tm * (2 * d * 4 + (R_WIDTH + M_WIDTH) * 2 * 2 + 2 * d * 2)
    return pl.pallas_call(
        _merge_kernel,
        grid=(bsz, length // tm),
        in_specs=[tok(d, 0), tok(R_WIDTH, 0), tok(M_WIDTH, 0), tok(R_WIDTH, rg_blk), tok(M_WIDTH, mo_blk),
                  tok(d, gr_blk), tok(d, gm_blk),
                  pl.BlockSpec((1, 1, d), lambda b, i: (b, 0, 0)),
                  const((R_WIDTH, d)), const((M_WIDTH, d)), const((d, d))],
        out_specs=tok(d, 0),
        out_shape=jax.ShapeDtypeStruct((bsz, length, d), F32),
        compiler_params=_params(("arbitrary", "arbitrary"), weights + tiles + 12 * MIB),
        name="merge",
    )(x, yr, ym, p, p, p, p, gt, wr, wm, wo)


def _sort_pairs(n):
    pairs = []
    p = 1
    while p < n:
        k = p
        while k >= 1:
            for j in range(k % p, n - k, 2 * k):
                for i in range(min(k, n - j - k)):
                    if (i + j) // (2 * p) == (i + j + k) // (2 * p):
                        pairs.append((i + j, i + j + k))
            k //= 2
        p *= 2
    return tuple(pairs)


def _bitonic_merge_pairs(n):
    pairs = []
    k = n // 2
    while k >= 1:
        pairs += [(i, i + k) for i in range(n) if (i & k) == 0]
        k //= 2
    return tuple(pairs)


_SORT_TOPK = _sort_pairs(P_TOPK)
_MERGE_TOPK = _bitonic_merge_pairs(P_TOPK)


def _apply_network(vals, pairs):
    vals = list(vals)
    for i, j in pairs:
        vals[i], vals[j] = jnp.maximum(vals[i], vals[j]), jnp.minimum(vals[i], vals[j])
    return vals


def _merge_top(a, b):
    n = len(a)
    return _apply_network([jnp.maximum(a[i], b[n - 1 - i]) for i in range(n)], _MERGE_TOPK)


def _tree(op, vals):
    vals = list(vals)
    while len(vals) > 1:
        vals = [op(vals[i], vals[i + 1]) for i in range(0, len(vals) - 1, 2)] + ([vals[-1]] if len(vals) % 2 else [])
    return vals[0]


ROW_MAX1, ROW_MIN1, ROW_MAX2, ROW_MIN2, ROW_RZ, ROW_TOP1, ROW_THR = 0, 1, 2, 3, 4, 8, 8 + P_TOPK
STAT_ROWS = 8 + 2 * P_TOPK


def _peer_select_kernel(x_ref, g_ref, sh_ref, sc_ref, wq_ref, keys_ref, h_ref, sel_ref, tile_s, grp_s, row_s):
    tm = x_ref.shape[1]
    n_col = tm // LANES
    assert 2 * n_col == 8
    neg_inf = jnp.float32(-jnp.inf)
    pos_inf = jnp.float32(jnp.inf)
    x = x_ref[0]
    y = x * lax.rsqrt(jnp.mean(x * x, axis=-1, keepdims=True) + EPS) * g_ref[...]
    hf = y * (1.0 + sc_ref[0]) + sh_ref[0]
    hb = hf.astype(BF16)
    h_ref[0] = hf.T.astype(BF16)
    query = _dot(hb, wq_ref[...]).astype(BF16)
    for hp in range(2 * P_HEADS):
        s = _dot_nt(keys_ref[hp], query[:, hp * P_NKEYS:(hp + 1) * P_NKEYS])
        for jc in range(n_col):
            blk = (hp % 2) * n_col + jc
            tile_s[hp // 2, blk * P_NKEYS:(blk + 1) * P_NKEYS, :] = s[:, jc * LANES:(jc + 1) * LANES]

    n_groups = P_NKEYS // P_TOPK
    pad_tile = jnp.full((8, LANES), neg_inf, F32)

    def head_body(h, carry):
        for g in range(n_groups):
            vals = [tile_s[h, pl.ds(g * P_TOPK + i, 8, stride=P_NKEYS), :] for i in range(P_TOPK)]
            vals = _apply_network(vals, _SORT_TOPK)
            for i in range(P_TOPK):
                grp_s[g * P_TOPK + i] = vals[i]
        for dst, src in ((0, 1), (2, 3), (4, 5), (6, 7), (0, 2), (4, 6), (0, 4)):
            merged = _merge_top([grp_s[dst * P_TOPK + i] for i in range(P_TOPK)],
                                [grp_s[src * P_TOPK + i] for i in range(P_TOPK)])
            for i in range(P_TOPK):
                grp_s[dst * P_TOPK + i] = merged[i]
        top = [grp_s[i] for i in range(P_TOPK)]
        other = [pltpu.roll(t, n_col, axis=0) for t in top]

        cand = {(a, b): top[a] + other[b] for a in range(P_TOPK) for b in range(P_TOPK // (a + 1))}
        row0 = [cand[(0, b)] for b in range(P_TOPK)]
        rest = [v for (a, _), v in cand.items() if a > 0]
        rest += [pad_tile] * (-len(rest) % P_TOPK)
        groups = [_apply_network(rest[g * P_TOPK:(g + 1) * P_TOPK], _SORT_TOPK) for g in range(len(rest) // P_TOPK)]
        best_rest = groups[0]
        for grp in groups[1:]:
            best_rest = _merge_top(best_rest, grp)
        final = [jnp.maximum(row0[i], best_rest[P_TOPK - 1 - i]) for i in range(P_TOPK)]
        tau = _tree(jnp.minimum, final)
        best = row0[0]
        z = _tree(jnp.add, [jnp.exp(v - best) for v in final])
        thr = [_tree(jnp.minimum, [jnp.where(cand[(a, b)] >= tau, other[b], pos_inf)
                                   for b in range(P_TOPK // (a + 1))]) for a in range(P_TOPK)]

        def put_row(r, v):
            for jc in range(n_col):
                row_s[r:r + 1, jc * LANES:(jc + 1) * LANES] = v[jc:jc + 1, :]

        put_row(ROW_MAX1, top[0])
        put_row(ROW_MIN1, top[P_TOPK - 1])
        put_row(ROW_MAX2, other[0])
        put_row(ROW_MIN2, other[P_TOPK - 1])
        put_row(ROW_RZ, 1.0 / z)
        for a in range(P_TOPK):
            put_row(ROW_TOP1 + a, top[a])
            put_row(ROW_THR + a, thr[a])

        for jc in range(n_col):
            cols = slice(jc * LANES, (jc + 1) * LANES)
            s1 = tile_s[h, jc * P_NKEYS:(jc + 1) * P_NKEYS, :]
            s2 = tile_s[h, (n_col + jc) * P_NKEYS:(n_col + jc + 1) * P_NKEYS, :]
            sel1 = s1 >= row_s[ROW_MIN1:ROW_MIN1 + 1, cols]
            sel2 = s2 >= row_s[ROW_MIN2:ROW_MIN2 + 1, cols]
            theta = jnp.full((P_NKEYS, LANES), pos_inf, F32)
            for a in range(P_TOPK):
                theta = jnp.where(s1 == row_s[ROW_TOP1 + a:ROW_TOP1 + a + 1, cols],
                                  row_s[ROW_THR + a:ROW_THR + a + 1, cols], theta)
            sel_ref[0, h, SEL_S2:SEL_S2 + P_NKEYS, cols] = jnp.where(sel2, s2, neg_inf)
            sel_ref[0, h, SEL_TH:SEL_TH + P_NKEYS, cols] = theta
            sel_ref[0, h, SEL_E1:SEL_E1 + P_NKEYS, cols] = jnp.where(
                sel1, jnp.exp(s1 - row_s[ROW_MAX1:ROW_MAX1 + 1, cols]), 0.0)
            sel_ref[0, h, SEL_E2:SEL_E2 + P_NKEYS, cols] = jnp.where(
                sel2, jnp.exp(s2 - row_s[ROW_MAX2:ROW_MAX2 + 1, cols]) * row_s[ROW_RZ:ROW_RZ + 1, cols], 0.0)
        return carry

    lax.fori_loop(0, P_HEADS, head_body, 0)


def _peer_select(x1, g, sh, sc, wq, keys, tm):
    bsz, length, d = x1.shape
    tiles = length // tm
    vmem = (2 * tm * d * 4 + d * d * 2 + 2 * tm * d * 2 + 2 * P_HEADS * SEL_ROWS * tm * 4
            + 2 * P_HEADS * P_NKEYS * tm * 4 + 3 * tm * d * 4 + 6 * MIB)
    resident = pl.Buffered(1)
    return pl.pallas_call(
        _peer_select_kernel,
        grid=(bsz, tiles),
        in_specs=[pl.BlockSpec((1, tm, d), lambda b, i: (b, i, 0)),
                  pl.BlockSpec((1, d), lambda b, i: (0, 0)),
                  pl.BlockSpec((1, 1, d), lambda b, i: (b, 0, 0)),
                  pl.BlockSpec((1, 1, d), lambda b, i: (b, 0, 0)),
                  pl.BlockSpec((d, d), lambda b, i: (0, 0), pipeline_mode=resident),
                  pl.BlockSpec((2 * P_HEADS, P_NKEYS, P_NKEYS), lambda b, i: (0, 0, 0), pipeline_mode=resident)],
        out_specs=[pl.BlockSpec((1, d, tm), lambda b, i: (b, 0, i)),
                   pl.BlockSpec((1, P_HEADS, SEL_ROWS, tm), lambda b, i: (b * tiles + i, 0, 0, 0))],
        out_shape=[jax.ShapeDtypeStruct((bsz, d, length), BF16),
                   jax.ShapeDtypeStruct((bsz * tiles, P_HEADS, SEL_ROWS, tm), F32)],
        scratch_shapes=[pltpu.VMEM((P_HEADS, 2 * tm, LANES), F32),
                        pltpu.VMEM((P_NKEYS, 8, LANES), F32),
                        pltpu.VMEM((STAT_ROWS, tm), F32)],
        compiler_params=_params(("arbitrary", "arbitrary"), vmem),
        name="peer_select",
    )(x1, g, sh, sc, wq, keys)


def _peer_expert_kernel(h_ref, sel_ref, down_ref, up_ref, x_ref, gt_ref, fg_ref, o_ref, acc, coef_s, *, n_steps):
    j = pl.program_id(2)
    chunk_experts = down_ref.shape[0]
    tm = h_ref.shape[2]
    n_kk = chunk_experts // P_NKEYS

    @pl.when(j == 0)
    def _():
        acc[...] = jnp.zeros_like(acc)

    a = _dot(down_ref[...], h_ref[0])
    for kk in range(n_kk):
        k1 = j * n_kk + kk
        rows = slice(kk * P_NKEYS, (kk + 1) * P_NKEYS)
        th_rows = [sel_ref[0, h, pl.ds(SEL_TH + k1, 1), :] for h in range(P_HEADS)]
        e1_rows = [sel_ref[0, h, pl.ds(SEL_E1 + k1, 1), :] for h in range(P_HEADS)]
        for tc in range(tm // LANES):
            cols = slice(tc * LANES, (tc + 1) * LANES)
            w = jnp.zeros((P_NKEYS, LANES), F32)
            for h in range(P_HEADS):
                picked = sel_ref[0, h, SEL_S2:SEL_S2 + P_NKEYS, cols] >= th_rows[h][:, cols]
                w = w + jnp.where(picked, sel_ref[0, h, SEL_E2:SEL_E2 + P_NKEYS, cols], 0.0) * e1_rows[h][:, cols]
            blk = a[rows, cols]
            act = 0.5 * blk * (1.0 + lax.erf(blk * (2.0 ** -0.5)))
            coef_s[cols, rows] = (act * w).T.astype(BF16)
    acc[...] += _dot(coef_s[...], up_ref[...])

    @pl.when(j == n_steps - 1)
    def _():
        x2 = x_ref[0] + gt_ref[0] * acc[...]
        o_ref[0] = x2 * lax.rsqrt(jnp.mean(x2 * x2, axis=-1, keepdims=True) + EPS) * fg_ref[...]


def _peer_expert(h2, sel, down, up, x1, gt, final_g, tm):
    bsz, length, d = x1.shape
    tiles = length // tm
    ec = 512
    n_steps = P_EXPERTS // ec
    vmem = (2 * tm * d * 2 + 2 * P_HEADS * SEL_ROWS * tm * 4 + 4 * ec * d * 2 + 4 * tm * d * 4 + tm * d * 4
            + 3 * ec * tm * 4 + 4 * MIB)
    return pl.pallas_call(
        functools.partial(_peer_expert_kernel, n_steps=n_steps),
        grid=(bsz, tiles, n_steps),
        in_specs=[pl.BlockSpec((1, d, tm), lambda b, i, j: (b, 0, i)),
                  pl.BlockSpec((1, P_HEADS, SEL_ROWS, tm), lambda b, i, j: (b * tiles + i, 0, 0, 0)),
                  pl.BlockSpec((ec, d), lambda b, i, j: (j, 0)),
                  pl.BlockSpec((ec, d), lambda b, i, j: (j, 0)),
                  pl.BlockSpec((1, tm, d), lambda b, i, j: (b, i, 0)),
                  pl.BlockSpec((1, 1, d), lambda b, i, j: (b, 0, 0)),
                  pl.BlockSpec((1, d), lambda b, i, j: (0, 0))],
        out_specs=pl.BlockSpec((1, tm, d), lambda b, i, j: (b, i, 0)),
        out_shape=jax.ShapeDtypeStruct((bsz, length, d), F32),
        scratch_shapes=[pltpu.VMEM((tm, d), F32),
                        pltpu.VMEM((tm, ec), BF16)],
        compiler_params=_params(("arbitrary", "arbitrary", "arbitrary"), vmem),
        name="peer_expert",
    )(h2, sel, down, up, x1, gt, final_g)


def _rope_tables(length):
    rows = length // GRID_W
    row = jnp.repeat(jnp.arange(rows, dtype=F32), GRID_W)
    col = jnp.tile(jnp.arange(GRID_W, dtype=F32), rows)
    n_ax = R_DK // 4
    freq = ROPE_BASE ** (-jnp.arange(n_ax, dtype=F32) / n_ax)
    ang = jnp.concatenate([row[:, None] * freq, col[:, None] * freq], axis=-1)
    cos = jnp.tile(jnp.cos(ang), (1, LANES // (R_DK // 2)))
    sin = jnp.sin(ang)
    sin_signed = jnp.tile(jnp.concatenate([-sin, sin], axis=-1), (1, LANES // R_DK))
    return cos, sin_signed


def _layer(x, ctx, c, c_ctx, w_ada, b_ada, norm1_g, w_in, ret_decay, m_conv, m_gate_bias, w_ret_out, w_mlstm_out,
           w_out, norm2_g, peer_query, peer_keys, peer_down, peer_up, final_g):
    bsz, length, d = x.shape

    c_rows = jnp.zeros((8, d), F32).at[:bsz].set(c).at[bsz].set(c_ctx)
    mod = _ada(c_rows, w_ada, b_ada)
    sh1, sc1, gt1, sh2, sc2, gt2 = [m[:bsz, None, :] for m in jnp.split(mod, ADA_CHUNKS, axis=-1)]
    csh1, csc1 = [jnp.broadcast_to(m[bsz][None, None, :], (bsz, 1, d))
                  for m in jnp.split(mod, ADA_CHUNKS, axis=-1)[:2]]

    o_rq, o_rk, o_rv, o_rg = 0, R_QK, 2 * R_QK, 2 * R_QK + R_WIDTH
    o_mq = o_rg + R_WIDTH
    o_mk, o_mv, o_mo = o_mq + M_QK, o_mq + 2 * M_QK, o_mq + 2 * M_QK + M_WIDTH
    o_mg = o_mo + M_WIDTH
    o_gr = o_mg + M_N_GATES
    wb = w_in.astype(BF16)
    w_p = jnp.concatenate([wb[:, o_rq:o_rg], wb[:, o_mq:o_mo], wb[:, o_rg:o_mq], wb[:, o_mo:o_mg], wb[:, o_gr:]],
                          axis=1)
    w_g = jnp.pad(wb[:, o_mg:o_gr], ((0, 0), (0, LANES - M_N_GATES)))
    gate_bias = jnp.pad(m_gate_bias.reshape(1, M_N_GATES), ((0, 0), (0, LANES - M_N_GATES)))
    g1 = norm1_g.reshape(1, d)

    p_ctx, gates_ctx = _inproj(ctx, g1, csh1, csc1, w_p, w_g, STATE_COLS, None)
    p_lat, gates_lat = _inproj(x, g1, sh1, sc1, w_p, w_g, w_p.shape[1], _rope_tables(length))

    zero_s = jnp.zeros((bsz, R_HEADS // 2, 2 * R_DK, 2 * R_DV), F32)
    sf0, sb0 = _ret_scan(ret_decay, p_ctx, zero_s, zero_s, with_out=False)
    yr = _ret_scan(ret_decay, p_lat, sf0, sb0, with_out=True)

    zero_c = jnp.zeros((bsz, M_HEADS, M_DK, M_DV), F32)
    zero_r = jnp.zeros((bsz, M_HEADS, 1, LANES), F32)
    m_states = _mlstm_scan(p_ctx, gates_ctx, m_conv, gate_bias,
                           (zero_c, zero_c, zero_r, zero_r, zero_r, zero_r), with_out=False)
    ym = _mlstm_scan(p_lat, gates_lat, m_conv, gate_bias, m_states, with_out=True)

    x1 = _merge(x, yr, ym, p_lat, gt1, w_ret_out.astype(BF16), w_mlstm_out.astype(BF16), w_out.astype(BF16))

    tm = 512
    keys = peer_keys.reshape(2 * P_HEADS, P_NKEYS, P_NKEYS).astype(BF16)
    h2, sel = _peer_select(x1, norm2_g.reshape(1, d), sh2, sc2, peer_query.astype(BF16), keys, tm)
    return _peer_expert(h2, sel, peer_down.astype(BF16), peer_up.astype(BF16), x1, gt2, final_g.reshape(1, d), tm)


def kernel(x, c, ctx, c_ctx, w_ada, b_ada, norm1_g, w_in, ret_decay, m_conv, m_gate_bias, w_ret_out, w_mlstm_out,
           w_out, norm2_g, peer_query, peer_keys, peer_down, peer_up, final_g):
    assert w_ada.shape[0] == 1, "single-layer block: the context stream is read, never updated"
    return _layer(x, ctx, c, c_ctx, w_ada[0], b_ada[0], norm1_g[0], w_in[0], ret_decay[0], m_conv[0],
                  m_gate_bias[0], w_ret_out[0], w_mlstm_out[0], w_out[0], norm2_g[0], peer_query[0], peer_keys[0],
                  peer_down[0], peer_up[0], final_g)
```
